```python
import jax, jax.numpy as jnp
from jax import lax
import numpy as np

D_MODEL = 1024
BATCH = 1
SEQ = 16384
DEPTH = 2

N_HEADS = 16
HEAD_DIM = D_MODEL // N_HEADS
HD = N_HEADS * HEAD_DIM
D_FF = ((8 * D_MODEL + 3 * 256 - 1) // (3 * 256)) * 256
PLE_DIM = 256
BLOCK_Q = 128
N_A = DEPTH // 2
N_B = DEPTH - N_A
EPS = 1e-6

kernel_name = 'yoco_stickbreak_fox_hybrid'


def rms_norm(x, g):
    xf = x.astype(jnp.float32)
    y = xf * lax.rsqrt(jnp.mean(xf * xf, axis=-1, keepdims=True) + EPS)
    return (y * g.astype(jnp.float32)).astype(x.dtype)


def split_heads(t):
    b, s, _ = t.shape
    return t.reshape(b, s, N_HEADS, HEAD_DIM).transpose(0, 2, 1, 3)


def merge_heads(t):
    b, h, s, d = t.shape
    return t.transpose(0, 2, 1, 3).reshape(b, s, h * d)


def to_blocks(t):
    b, h, s = t.shape[:3]
    t = t.reshape((b, h, s // BLOCK_Q, BLOCK_Q) + t.shape[3:])
    return jnp.moveaxis(t, 2, 0)


def from_blocks(t):
    t = jnp.moveaxis(t, 0, 2)
    b, h, nb, bq, d = t.shape
    return t.reshape(b, h, nb * bq, d)


def stick_breaking_attention(q, k, v):
    s_len = q.shape[2]
    nb = s_len // BLOCK_Q
    kpos = jnp.arange(s_len)
    scale = HEAD_DIM ** -0.5

    def block(args):
        qb, i = args
        tpos = i * BLOCK_Q + jnp.arange(BLOCK_Q)
        z = jnp.einsum('bhqd,bhkd->bhqk', qb, k).astype(jnp.float32) * scale
        strict = kpos[None, :] < tpos[:, None]
        log_one_minus = jnp.where(strict, -jax.nn.softplus(z), 0.0)
        later = lax.cumsum(log_one_minus, axis=3, reverse=True) - log_one_minus
        w = jnp.where(strict, jnp.exp(jax.nn.log_sigmoid(z) + later), 0.0)
        return jnp.einsum('bhqk,bhkd->bhqd', w.astype(v.dtype), v)

    out = lax.map(block, (to_blocks(q), jnp.arange(nb)))
    return from_blocks(out)


def forgetting_attention(q, k, v, f_cum):
    s_len = q.shape[2]
    nb = s_len // BLOCK_Q
    kpos = jnp.arange(s_len)
    scale = HEAD_DIM ** -0.5

    def block(args):
        qb, fb, i = args
        tpos = i * BLOCK_Q + jnp.arange(BLOCK_Q)
        logits = (jnp.einsum('bhqd,bhkd->bhqk', qb, k).astype(jnp.float32) * scale
                  + fb[..., :, None] - f_cum[:, :, None, :])
        causal = kpos[None, :] <= tpos[:, None]
        logits = jnp.where(causal, logits, -jnp.inf)
        probs = jax.nn.softmax(logits, axis=-1)
        return jnp.einsum('bhqk,bhkd->bhqd', probs.astype(v.dtype), v)

    out = lax.map(block, (to_blocks(q), to_blocks(f_cum), jnp.arange(nb)))
    return from_blocks(out)


def swiglu(h, w_gu, w_d):
    g, u = jnp.split(h @ w_gu, 2, axis=-1)
    return (jax.nn.silu(g) * u) @ w_d


def setup_inputs(seed: int = 0) -> dict:
    key = jax.random.key(seed)
    ks = jax.random.split(key, 20)
    f32 = jnp.float32

    def w(k, shape, fan_in, extra=1.0):
        return jax.random.normal(k, shape, f32) * (fan_in ** -0.5) * extra

    def gain(k, shape):
        return 1.0 + 0.05 * jax.random.normal(k, shape, f32)

    res_scale = (2.0 * DEPTH) ** -0.5
    return {
        'x': jax.random.normal(ks[0], (BATCH, SEQ, D_MODEL), f32),
        'p': jax.random.normal(ks[1], (DEPTH, BATCH, SEQ, PLE_DIM), f32),
        'attn_norm_g': gain(ks[2], (DEPTH, D_MODEL)),
        'sb_w_qkv': w(ks[3], (N_A, D_MODEL, 3 * HD), D_MODEL),
        'sb_w_o': w(ks[4], (N_A, HD, D_MODEL), HD, res_scale),
        'shared_norm_g': gain(ks[5], (D_MODEL,)),
        'shared_w_kvf': w(ks[6], (D_MODEL, 2 * HD + N_HEADS), D_MODEL),
        'shared_b_f': 0.1 * jax.random.normal(ks[7], (N_HEADS,), f32),
        'shared_k_norm_g': gain(ks[8], (HEAD_DIM,)),
        'fox_w_q': w(ks[9], (N_B, D_MODEL, HD), D_MODEL),
        'fox_q_norm_g': gain(ks[10], (N_B, HEAD_DIM)),
        'fox_w_o': w(ks[11], (N_B, HD, D_MODEL), HD, res_scale),
        'ffn_norm_g': gain(ks[12], (DEPTH, D_MODEL)),
        'ffn_w_gu': w(ks[13], (DEPTH, D_MODEL, 2 * D_FF), D_MODEL),
        'ffn_w_d': w(ks[14], (DEPTH, D_FF, D_MODEL), D_FF, res_scale),
        'ple_norm_g': gain(ks[15], (DEPTH, D_MODEL)),
        'ple_w_gate': w(ks[16], (DEPTH, D_MODEL, D_MODEL), D_MODEL),
        'ple_w_proj': w(ks[17], (DEPTH, PLE_DIM, D_MODEL), PLE_DIM, res_scale),
    }


def reference(x, p, attn_norm_g, sb_w_qkv, sb_w_o, shared_norm_g, shared_w_kvf, shared_b_f,
              shared_k_norm_g, fox_w_q, fox_q_norm_g, fox_w_o, ffn_norm_g, ffn_w_gu, ffn_w_d,
              ple_norm_g, ple_w_gate, ple_w_proj):
    h = x
    k_sh = v_sh = f_cum = None
    for i in range(DEPTH):
        hn = rms_norm(h, attn_norm_g[i])
        if i < N_A:
            q, k, v = jnp.split(hn @ sb_w_qkv[i], 3, axis=-1)
            mix = stick_breaking_attention(split_heads(q), split_heads(k), split_heads(v))
            h = h + merge_heads(mix) @ sb_w_o[i]
        else:
            if i == N_A:
                hs = rms_norm(h, shared_norm_g)
                kvf = hs @ shared_w_kvf
                k_sh = rms_norm(split_heads(kvf[..., :HD]), shared_k_norm_g)
                v_sh = split_heads(kvf[..., HD:2 * HD])
                f_logit = kvf[..., 2 * HD:].astype(jnp.float32) + shared_b_f.astype(jnp.float32)
                f_cum = lax.cumsum(jax.nn.log_sigmoid(f_logit), axis=1).transpose(0, 2, 1)
            j = i - N_A
            q = rms_norm(split_heads(hn @ fox_w_q[j]), fox_q_norm_g[j])
            mix = forgetting_attention(q, k_sh, v_sh, f_cum)
            h = h + merge_heads(mix) @ fox_w_o[j]
        h = h + swiglu(rms_norm(h, ffn_norm_g[i]), ffn_w_gu[i], ffn_w_d[i])
        gate = jax.nn.sigmoid(rms_norm(h, ple_norm_g[i]) @ ple_w_gate[i])
        h = h + gate * (p[i] @ ple_w_proj[i])
    return h
```

```python
import functools

import jax
import jax.numpy as jnp
from jax import lax
from jax.experimental import pallas as pl
from jax.experimental.pallas import tpu as pltpu

D_MODEL = 1024
SEQ = 16384
N_HEADS = 16
HEAD_DIM = 64
HD = N_HEADS * HEAD_DIM
D_FF = 2816
PLE_DIM = 256
EPS = 1e-6
SCALE = HEAD_DIM ** -0.5

LANES = 128
PAIR = 2 * HEAD_DIM
N_PAIRS = N_HEADS // 2
MXU_DIM = 256
EXP_UNDERFLOW = 104.0
NEG_BIG = -1e30

ROW_TILE = 512
ATTN_TILE = 256
FF_CHUNK = MXU_DIM
VMEM_LIMIT = 56 * 1024 * 1024

F32 = jnp.float32
BF16 = jnp.bfloat16


def _dot(a, b):
    return jnp.dot(a, b, preferred_element_type=F32)


def _dot_nt(a, b):
    return lax.dot_general(a, b, (((1,), (1,)), ((), ())), preferred_element_type=F32)


def _rms_scale(x):
    return lax.rsqrt(jnp.mean(x * x, axis=-1, keepdims=True) + EPS)


def _sigmoid(x):
    return 1.0 / (1.0 + jnp.exp(-x))


def _log1p_exp_neg_abs(x):
    return jnp.log(1.0 + jnp.exp(-jnp.abs(x)))


def _resident(shape):
    return pl.BlockSpec(shape, lambda *_: (0,) * len(shape), pipeline_mode=pl.Buffered(1))


def _head_rms_norm(y, gain_pair):
    lane = lax.broadcasted_iota(jnp.int32, (1, PAIR), 1)
    first = lane < HEAD_DIM
    out = []
    for p in range(N_PAIRS):
        yp = y[:, p * PAIR:(p + 1) * PAIR]
        sq = yp * yp
        tot = jnp.sum(sq, axis=-1, keepdims=True)
        s0 = jnp.sum(jnp.where(first, sq, 0.0), axis=-1, keepdims=True)
        ms = jnp.where(first, s0, tot - s0) * (1.0 / HEAD_DIM)
        out.append(yp * lax.rsqrt(ms + EPS) * gain_pair)
    return out


def _qkv_proj_kernel(x_ref, g_ref, w_ref, q_ref, k_ref, v_ref):
    x = x_ref[...]
    hn = (x * _rms_scale(x) * g_ref[...]).astype(BF16)
    q_ref[...] = (_dot(hn, w_ref[:, 0:HD]) * SCALE).astype(BF16)
    k_ref[...] = _dot(hn, w_ref[:, HD:2 * HD]).astype(BF16)
    v_ref[...] = _dot(hn, w_ref[:, 2 * HD:3 * HD]).astype(BF16)


def _qkv_proj(x, g, w):
    rows = pl.BlockSpec((ROW_TILE, D_MODEL), lambda i: (i, 0))
    out = jax.ShapeDtypeStruct((SEQ, HD), BF16)
    return pl.pallas_call(
        _qkv_proj_kernel,
        grid=(SEQ // ROW_TILE,),
        in_specs=[rows, _resident((1, D_MODEL)), _resident((D_MODEL, 3 * HD))],
        out_specs=[pl.BlockSpec((ROW_TILE, HD), lambda i: (i, 0))] * 3,
        out_shape=[out] * 3,
        compiler_params=pltpu.CompilerParams(
            dimension_semantics=("arbitrary",), vmem_limit_bytes=VMEM_LIMIT),
        name="qkv_proj",
    )(x, g, w)


def _sb_attn_kernel(q_ref, k_ref, v_ref, tri_ref, o_ref, acc_ref, run_ref):
    t = ATTN_TILE
    qi = pl.program_id(1)
    lane = lax.broadcasted_iota(jnp.int32, (1, PAIR), 1)
    head_lanes = (lane < HEAD_DIM, lane >= HEAD_DIM)
    q = q_ref[...]
    q_heads = [jnp.where(m, q, jnp.zeros_like(q)) for m in head_lanes]
    tri = tri_ref[...]
    acc_ref[...] = jnp.zeros_like(acc_ref)
    run_ref[...] = jnp.zeros_like(run_ref)

    def visit(kb, diagonal):
        start = pl.multiple_of(kb * t, t)
        k_blk = k_ref[pl.ds(start, t), :]
        v_blk = v_ref[pl.ds(start, t), :]
        if diagonal:
            strict = (lax.broadcasted_iota(jnp.int32, (t, t), 1)
                      < lax.broadcasted_iota(jnp.int32, (t, t), 0))
        pv = []
        for h in range(2):
            z = _dot_nt(q_heads[h], k_blk)
            tail = _log1p_exp_neg_abs(z)
            log_keep = -(jnp.maximum(z, 0.0) + tail)
            log_beta = jnp.minimum(z, 0.0) - tail
            if diagonal:
                log_keep = jnp.where(strict, log_keep, 0.0)
            hi = log_keep.astype(BF16)
            lo = (log_keep - hi.astype(F32)).astype(BF16)
            suffix = _dot(hi, tri) + _dot(lo, tri)
            run = run_ref[h]
            w = jnp.exp(log_beta + (suffix - log_keep) + run)
            if diagonal:
                w = jnp.where(strict, w, 0.0)
            pv.append(_dot(w.astype(BF16), v_blk))
            run_ref[h] = run + suffix[:, 0:1]
        acc_ref[...] += jnp.where(head_lanes[0], pv[0], pv[1])
        return jnp.max(jnp.maximum(run_ref[0], run_ref[1])) > -EXP_UNDERFLOW

    go = visit(qi, True)
    lax.while_loop(lambda c: jnp.logical_and(c[0] >= 0, c[1]),
                   lambda c: (c[0] - 1, visit(c[0], False)),
                   (qi - 1, go))
    o_ref[...] = acc_ref[...].astype(o_ref.dtype)


def _sb_attn(q, k, v):
    t = ATTN_TILE
    tri = jnp.tri(t, dtype=BF16)
    blk = pl.BlockSpec((t, PAIR), lambda p, i: (i, p))
    seq = pl.BlockSpec((SEQ, PAIR), lambda p, i: (0, p))
    return pl.pallas_call(
        _sb_attn_kernel,
        grid=(N_PAIRS, SEQ // t),
        in_specs=[blk, seq, seq, _resident((t, t))],
        out_specs=blk,
        out_shape=jax.ShapeDtypeStruct((SEQ, HD), BF16),
        scratch_shapes=[pltpu.VMEM((t, PAIR), F32), pltpu.VMEM((2, t, 1), F32)],
        compiler_params=pltpu.CompilerParams(
            dimension_semantics=("arbitrary", "arbitrary"), vmem_limit_bytes=VMEM_LIMIT),
        name="sb_attn",
    )(q, k, v, tri)


def _fox_attn_kernel(gq_ref, gk_ref, q_ref, k_ref, v_ref, f_ref, o_ref,
                     acc_ref, m_ref, l_ref, fq_ref):
    t = ATTN_TILE
    pair = pl.program_id(0)
    qi = pl.program_id(1)
    q_start = pl.multiple_of(qi * t, t)
    lane = lax.broadcasted_iota(jnp.int32, (1, PAIR), 1)
    head_lanes = (lane < HEAD_DIM, lane >= HEAD_DIM)
    q = q_ref[...]
    q_heads = [jnp.where(m, q, jnp.zeros_like(q)) for m in head_lanes]
    row = lax.broadcasted_iota(jnp.int32, (t, t), 0)
    col = lax.broadcasted_iota(jnp.int32, (t, t), 1)
    logit_bound = (jnp.max(jnp.abs(gq_ref[...])) * jnp.max(jnp.abs(gk_ref[...]))
                   * (HEAD_DIM * SCALE * 1.02) + 0.1)

    acc_ref[...] = jnp.zeros_like(acc_ref)
    l_ref[...] = jnp.zeros_like(l_ref)
    m_ref[...] = jnp.full_like(m_ref, NEG_BIG)
    for h in range(2):
        f_row = f_ref[pl.ds(2 * pair + h, 1), pl.ds(q_start, t)]
        fq_ref[h] = jnp.sum(jnp.where(row == col, jnp.broadcast_to(f_row, (t, t)), 0.0),
                            axis=-1, keepdims=True)

    def visit(kb, diagonal):
        start = pl.multiple_of(kb * t, t)
        k_blk = k_ref[pl.ds(start, t), :]
        v_blk = v_ref[pl.ds(start, t), :]
        slack = []
        for h in range(2):
            fq = fq_ref[h]
            fk = f_ref[pl.ds(2 * pair + h, 1), pl.ds(start, t)]
            s = _dot_nt(q_heads[h], k_blk) + (fq - fk)
            if diagonal:
                s = jnp.where(col <= row, s, NEG_BIG)
            m_old = m_ref[h]
            m_new = jnp.maximum(m_old, jnp.max(s, axis=-1, keepdims=True))
            alpha = jnp.exp(m_old - m_new)
            p = jnp.exp(s - m_new)
            l_ref[h] = alpha * l_ref[h] + jnp.sum(p, axis=-1, keepdims=True)
            acc_ref[h] = alpha * acc_ref[h] + _dot(p.astype(BF16), v_blk)
            m_ref[h] = m_new
            slack.append(jnp.max(logit_bound + fq - jnp.max(fk) - m_new))
        return jnp.maximum(slack[0], slack[1]) > -EXP_UNDERFLOW

    go = visit(qi, True)
    lax.while_loop(lambda c: jnp.logical_and(c[0] >= 0, c[1]),
                   lambda c: (c[0] - 1, visit(c[0], False)),
                   (qi - 1, go))
    o_ref[...] = jnp.where(head_lanes[0], acc_ref[0] / l_ref[0],
                           acc_ref[1] / l_ref[1]).astype(o_ref.dtype)


def _fox_attn(gq, gk, q, k, v, f):
    t = ATTN_TILE
    blk = pl.BlockSpec((t, PAIR), lambda p, i: (i, p))
    seq = pl.BlockSpec((SEQ, PAIR), lambda p, i: (0, p))
    return pl.pallas_call(
        _fox_attn_kernel,
        grid=(N_PAIRS, SEQ // t),
        in_specs=[_resident((1, HEAD_DIM)), _resident((1, HEAD_DIM)), blk, seq, seq,
                  _resident((N_HEADS, SEQ))],
        out_specs=blk,
        out_shape=jax.ShapeDtypeStruct((SEQ, HD), BF16),
        scratch_shapes=[pltpu.VMEM((2, t, PAIR), F32), pltpu.VMEM((2, t, 1), F32),
                        pltpu.VMEM((2, t, 1), F32), pltpu.VMEM((2, t, 1), F32)],
        compiler_params=pltpu.CompilerParams(
            dimension_semantics=("arbitrary", "arbitrary"), vmem_limit_bytes=VMEM_LIMIT),
        name="fox_attn",
    )(gq, gk, q, k, v, f)


def _post_attn_kernel(h_ref, mix_ref, p_ref, wo_ref, gf_ref, wgu_ref, wd_ref, gp_ref,
                      wgate_ref, wproj_ref, out_ref, acc_ref):
    h1 = h_ref[...] + _dot(mix_ref[...], wo_ref[...])
    hn = (h1 * _rms_scale(h1) * gf_ref[...]).astype(BF16)
    acc_ref[...] = h1
    for c in range(D_FF // FF_CHUNK):
        lo = c * FF_CHUNK
        g = _dot(hn, wgu_ref[:, lo:lo + FF_CHUNK])
        u = _dot(hn, wgu_ref[:, D_FF + lo:D_FF + lo + FF_CHUNK])
        act = (g * _sigmoid(g) * u).astype(BF16)
        acc_ref[...] += _dot(act, wd_ref[lo:lo + FF_CHUNK, :])
    h2 = acc_ref[...]
    hn = (h2 * _rms_scale(h2) * gp_ref[...]).astype(BF16)
    gate = _sigmoid(_dot(hn, wgate_ref[...]))
    emb = _dot(p_ref[...].astype(BF16), wproj_ref[...])
    out_ref[...] = h2 + gate * emb


def _post_attn(h, mix, p, wo, gf, wgu, wd, gp, wgate, wproj):
    rows = lambda n: pl.BlockSpec((ROW_TILE, n), lambda i: (i, 0))
    return pl.pallas_call(
        _post_attn_kernel,
        grid=(SEQ // ROW_TILE,),
        in_specs=[rows(D_MODEL), rows(HD), rows(PLE_DIM),
                  _resident((HD, D_MODEL)), _resident((1, D_MODEL)),
                  _resident((D_MODEL, 2 * D_FF)), _resident((D_FF, D_MODEL)),
                  _resident((1, D_MODEL)), _resident((D_MODEL, D_MODEL)),
                  _resident((PLE_DIM, D_MODEL))],
        out_specs=rows(D_MODEL),
        out_shape=jax.ShapeDtypeStruct((SEQ, D_MODEL), F32),
        scratch_shapes=[pltpu.VMEM((ROW_TILE, D_MODEL), F32)],
        compiler_params=pltpu.CompilerParams(
            dimension_semantics=("arbitrary",), vmem_limit_bytes=VMEM_LIMIT),
        name="post_attn",
    )(h, mix, p, wo, gf, wgu, wd, gp, wgate, wproj)


def _shared_proj_kernel(h_ref, ga_ref, gs_ref, wq_ref, wk_ref, wv_ref, wft_ref, bf_ref,
                        gq_ref, gk_ref, cum_ref, q_ref, k_ref, v_ref, f_ref, carry_ref):
    @pl.when(pl.program_id(0) == 0)
    def _():
        carry_ref[...] = jnp.zeros_like(carry_ref)

    h = h_ref[...]
    hn = h * _rms_scale(h)
    ha = (hn * ga_ref[...]).astype(BF16)
    hs = (hn * gs_ref[...]).astype(BF16)

    for p, y in enumerate(_head_rms_norm(_dot(ha, wq_ref[...]), gq_ref[...])):
        q_ref[:, p * PAIR:(p + 1) * PAIR] = (y * SCALE).astype(BF16)
    for p, y in enumerate(_head_rms_norm(_dot(hs, wk_ref[...]), gk_ref[...])):
        k_ref[:, p * PAIR:(p + 1) * PAIR] = y.astype(BF16)
    v_ref[...] = _dot(hs, wv_ref[...]).astype(BF16)

    f_logit = _dot_nt(wft_ref[...], hs) + bf_ref[...]
    log_f = jnp.minimum(f_logit, 0.0) - _log1p_exp_neg_abs(f_logit)
    a = log_f.astype(BF16)
    rest = log_f - a.astype(F32)
    b = rest.astype(BF16)
    c = (rest - b.astype(F32)).astype(BF16)
    parts = _dot(jnp.concatenate([a, b, c], axis=0), cum_ref[...])
    n = N_HEADS
    f_cum = carry_ref[...] + (parts[0:n] + parts[n:2 * n] + parts[2 * n:3 * n])
    f_ref[...] = f_cum
    carry_ref[...] = f_cum[:, ROW_TILE - 1:ROW_TILE]


def _shared_proj(h, ga, gs, wq, wk, wv, wft, bf, gq, gk):
    cum = jnp.triu(jnp.ones((ROW_TILE, ROW_TILE), BF16))
    rows = pl.BlockSpec((ROW_TILE, D_MODEL), lambda i: (i, 0))
    out = jax.ShapeDtypeStruct((SEQ, HD), BF16)
    return pl.pallas_call(
        _shared_proj_kernel,
        grid=(SEQ // ROW_TILE,),
        in_specs=[rows, _resident((1, D_MODEL)), _resident((1, D_MODEL)),
                  _resident((D_MODEL, HD)), _resident((D_MODEL, HD)), _resident((D_MODEL, HD)),
                  _resident((N_HEADS, D_MODEL)), _resident((N_HEADS, 1)),
                  _resident((1, PAIR)), _resident((1, PAIR)),
                  _resident((ROW_TILE, ROW_TILE))],
        out_specs=[rows, rows, rows, pl.BlockSpec((N_HEADS, ROW_TILE), lambda i: (0, i))],
        out_shape=[out, out, out, jax.ShapeDtypeStruct((N_HEADS, SEQ), F32)],
        scratch_shapes=[pltpu.VMEM((N_HEADS, 1), F32)],
        compiler_params=pltpu.CompilerParams(
            dimension_semantics=("arbitrary",), vmem_limit_bytes=VMEM_LIMIT),
        name="shared_proj",
    )(h, ga, gs, wq, wk, wv, wft, bf, gq, gk, cum)


def kernel(x, p, attn_norm_g, sb_w_qkv, sb_w_o, shared_norm_g, shared_w_kvf, shared_b_f,
           shared_k_norm_g, fox_w_q, fox_q_norm_g, fox_w_o, ffn_norm_g, ffn_w_gu, ffn_w_d,
           ple_norm_g, ple_w_gate, ple_w_proj):
    assert x.shape == (1, SEQ, D_MODEL) and p.shape == (2, 1, SEQ, PLE_DIM)
    assert sb_w_qkv.shape[0] == 1 and fox_w_q.shape[0] == 1
    row = lambda g: g.reshape(1, -1).astype(F32)
    bf = lambda w: w.astype(BF16)

    def post(h, mix, layer, wo):
        return _post_attn(h, mix, p[layer, 0], bf(wo), row(ffn_norm_g[layer]),
                          bf(ffn_w_gu[layer]), bf(ffn_w_d[layer]), row(ple_norm_g[layer]),
                          bf(ple_w_gate[layer]), bf(ple_w_proj[layer]))

    h = x[0]
    q, k, v = _qkv_proj(h, row(attn_norm_g[0]), bf(sb_w_qkv[0]))
    h = post(h, _sb_attn(q, k, v), 0, sb_w_o[0])
    pair_gain = lambda g: jnp.tile(g.reshape(1, HEAD_DIM).astype(F32), (1, 2))
    q, k, v, f = _shared_proj(
        h, row(attn_norm_g[1]), row(shared_norm_g), bf(fox_w_q[0]),
        bf(shared_w_kvf[:, 0:HD]), bf(shared_w_kvf[:, HD:2 * HD]),
        bf(shared_w_kvf[:, 2 * HD:].T), shared_b_f.reshape(N_HEADS, 1).astype(F32),
        pair_gain(fox_q_norm_g[0]), pair_gain(shared_k_norm_g))
    mix = _fox_attn(row(fox_q_norm_g[0]), row(shared_k_norm_g), q, k, v, f)
    h = post(h, mix, 1, fox_w_o[0])
    return h[None]
```
